```python
import jax, jax.numpy as jnp
from jax import lax
import numpy as np

D_MODEL = 1024
BATCH = 8
SEQ = 4096
DEPTH = 2
DEC_BATCH = 128
DEC_SEQ = 4
PAST_LEN = 16384
PAGE_SIZE = 128

POOL_WIDTH = D_MODEL // 2
POOL_WINDOWS = (2, 4, 8, 16)
POOL_GROUPS = len(POOL_WINDOWS)
POOL_GROUP_DIM = POOL_WIDTH // POOL_GROUPS
POOL_CTX = max(POOL_WINDOWS) - 1
MLA_HEADS = 8
QK_NOPE_DIM = 64
QK_ROPE_DIM = 32
V_HEAD_DIM = 64
MLA_WIDTH = MLA_HEADS * V_HEAD_DIM
Q_RANK = D_MODEL // 4
KV_RANK = D_MODEL // 8
ROPE_THETA = 10000.0
SM_SCALE = (QK_NOPE_DIM + QK_ROPE_DIM) ** -0.5
Q_BLOCK = 128
D_IN = POOL_WIDTH + Q_RANK + KV_RANK + QK_ROPE_DIM
D_MIX = POOL_WIDTH + MLA_WIDTH
N_MEM = 256
MEM_HEADS = 4
MEM_HEAD_DIM = D_MODEL // MEM_HEADS
D_FF = 4 * D_MODEL
ALPHA = (2.0 * DEPTH) ** 0.25
BETA = (8.0 * DEPTH) ** -0.25
LN_EPS = 1e-5
RMS_EPS = 1e-6

kernel_name = 'hymba_pool_mla_deepnorm_step'


def layer_norm(x, g, b):
    xf = x.astype(jnp.float32)
    mu = jnp.mean(xf, axis=-1, keepdims=True)
    var = jnp.mean(jnp.square(xf - mu), axis=-1, keepdims=True)
    y = (xf - mu) * lax.rsqrt(var + LN_EPS) * g.astype(jnp.float32) + b.astype(jnp.float32)
    return y.astype(x.dtype)


def rms_norm(x, g):
    xf = x.astype(jnp.float32)
    y = xf * lax.rsqrt(jnp.mean(jnp.square(xf), axis=-1, keepdims=True) + RMS_EPS) * g.astype(jnp.float32)
    return y.astype(x.dtype)


def rope(x, pos):
    half = QK_ROPE_DIM // 2
    inv_freq = ROPE_THETA ** (-jnp.arange(half, dtype=jnp.float32) / half)
    ang = pos.astype(jnp.float32)[:, None] * inv_freq[None, :]
    ang = ang.reshape((ang.shape[0],) + (1,) * (x.ndim - 3) + (half,))
    cos, sin = jnp.cos(ang), jnp.sin(ang)
    xf = x.astype(jnp.float32)
    x1, x2 = xf[..., :half], xf[..., half:]
    return jnp.concatenate([x1 * cos - x2 * sin, x1 * sin + x2 * cos], axis=-1).astype(x.dtype)


def pool_mix(u_ctx, u, pos, w_pool, pool_scale):
    B, T, _ = u.shape
    ext = jnp.concatenate([u_ctx, u], axis=1)
    cs = jnp.cumsum(ext.astype(jnp.float32), axis=1)
    cs = jnp.concatenate([jnp.zeros_like(cs[:, :1]), cs], axis=1)
    end = POOL_CTX + 1
    means = []
    for g, w in enumerate(POOL_WINDOWS):
        c0, c1 = g * POOL_GROUP_DIM, (g + 1) * POOL_GROUP_DIM
        s = cs[:, end:end + T, c0:c1] - cs[:, end - w:end - w + T, c0:c1]
        cnt = jnp.minimum(w, pos + 1).astype(jnp.float32)[None, :, None]
        means.append(s / cnt)
    pooled = jnp.concatenate(means, axis=-1) - u.astype(jnp.float32)
    pooled = pooled.reshape(B, T, POOL_GROUPS, POOL_GROUP_DIM).astype(u.dtype)
    y = jnp.einsum('btgc,gcd->btgd', pooled, w_pool).reshape(B, T, POOL_WIDTH)
    return y * pool_scale, ext[:, -POOL_CTX:]


def mla_attend_prompt(q_lat, q_rope, ckv, k_rope):
    B, S = q_lat.shape[:2]
    nblk = S // Q_BLOCK
    ql = q_lat.reshape(B, nblk, Q_BLOCK, MLA_HEADS, KV_RANK).transpose(1, 0, 2, 3, 4)
    qr = q_rope.reshape(B, nblk, Q_BLOCK, MLA_HEADS, QK_ROPE_DIM).transpose(1, 0, 2, 3, 4)
    starts = jnp.arange(nblk, dtype=jnp.int32) * Q_BLOCK
    kpos = jnp.arange(S, dtype=jnp.int32)

    def block(args):
        qlb, qrb, start = args
        s = jnp.einsum('bqhc,bkc->bhqk', qlb, ckv) + jnp.einsum('bqhr,bkr->bhqk', qrb, k_rope)
        s = s.astype(jnp.float32) * SM_SCALE
        qpos = start + jnp.arange(Q_BLOCK, dtype=jnp.int32)
        s = jnp.where(qpos[:, None] >= kpos[None, :], s, -jnp.inf)
        p = jax.nn.softmax(s, axis=-1).astype(ckv.dtype)
        return jnp.einsum('bhqk,bkc->bqhc', p, ckv)

    o = lax.map(block, (ql, qr, starts))
    return o.transpose(1, 0, 2, 3, 4).reshape(B, S, MLA_HEADS, KV_RANK)


def mla_attend_sample(q_lat, q_rope, ckv_new, krope_new, ckv_past, krope_past):
    T = q_lat.shape[1]
    P = ckv_past.shape[1]
    s_past = jnp.einsum('bqhc,bkc->bhqk', q_lat, ckv_past) + jnp.einsum('bqhr,bkr->bhqk', q_rope, krope_past)
    s_new = jnp.einsum('bqhc,bkc->bhqk', q_lat, ckv_new) + jnp.einsum('bqhr,bkr->bhqk', q_rope, krope_new)
    causal = jnp.arange(T)[:, None] >= jnp.arange(T)[None, :]
    s_new = jnp.where(causal, s_new.astype(jnp.float32) * SM_SCALE, -jnp.inf)
    s = jnp.concatenate([s_past.astype(jnp.float32) * SM_SCALE, s_new], axis=-1)
    p = jax.nn.softmax(s, axis=-1).astype(ckv_new.dtype)
    return (jnp.einsum('bhqk,bkc->bqhc', p[..., :P], ckv_past)
            + jnp.einsum('bhqk,bkc->bqhc', p[..., P:], ckv_new))


def token_mixer(x, pos, pool_ctx, past, w_in, g_q, g_kv, w_uq, w_uk, w_uv, w_pool, pool_scale, w_o):
    B, T, _ = x.shape
    h = x @ w_in
    o0 = POOL_WIDTH
    o1 = o0 + Q_RANK
    o2 = o1 + KV_RANK
    u, c_q, c_kv, k_r = h[..., :o0], h[..., o0:o1], h[..., o1:o2], h[..., o2:]
    pool_y, pool_state = pool_mix(pool_ctx, u, pos, w_pool, pool_scale)
    q = (rms_norm(c_q, g_q) @ w_uq).reshape(B, T, MLA_HEADS, QK_NOPE_DIM + QK_ROPE_DIM)
    q_lat = jnp.einsum('bthd,chd->bthc', q[..., :QK_NOPE_DIM], w_uk)
    q_rope = rope(q[..., QK_NOPE_DIM:], pos)
    ckv = rms_norm(c_kv, g_kv)
    k_rope = rope(k_r, pos)
    if past is None:
        o_lat = mla_attend_prompt(q_lat, q_rope, ckv, k_rope)
    else:
        o_lat = mla_attend_sample(q_lat, q_rope, ckv, k_rope, past[0], past[1])
    o = jnp.einsum('bthc,chd->bthd', o_lat, w_uv).reshape(B, T, MLA_WIDTH)
    y = jnp.concatenate([pool_y, o], axis=-1) @ w_o
    return y, pool_state, ckv, k_rope


def mem_attend(x, mem_k, mem_v, w_mq, w_mo):
    B, T, _ = x.shape
    q = (x @ w_mq).reshape(B, T, MEM_HEADS, MEM_HEAD_DIM)
    s = jnp.einsum('bthd,bmhd->bhtm', q, mem_k).astype(jnp.float32) * (MEM_HEAD_DIM ** -0.5)
    p = jax.nn.softmax(s, axis=-1).astype(x.dtype)
    o = jnp.einsum('bhtm,bmhd->bthd', p, mem_v).reshape(B, T, D_MODEL)
    return o @ w_mo


def sq_relu_mlp(x, w1, w2):
    return jnp.square(jax.nn.relu(x @ w1)) @ w2


def setup_inputs(seed: int = 0) -> dict:
    key = jax.random.key(seed)
    ks = jax.random.split(key, 32)
    f32 = jnp.float32

    def nrm(k, shape, scale=1.0):
        return jax.random.normal(k, shape, f32) * scale

    n_pages = PAST_LEN // PAGE_SIZE
    n_phys = (DEC_BATCH * n_pages * 5) // 4
    page_table = jax.random.permutation(ks[7], n_phys)[:DEC_BATCH * n_pages].reshape(DEC_BATCH, n_pages).astype(jnp.int32)
    return {
        'x_prompt': nrm(ks[0], (BATCH, SEQ, D_MODEL)),
        'x_sample': nrm(ks[1], (DEC_BATCH, DEC_SEQ, D_MODEL)),
        'cache_ckv': nrm(ks[2], (DEPTH, n_phys, PAGE_SIZE, KV_RANK)),
        'cache_krope': nrm(ks[3], (DEPTH, n_phys, PAGE_SIZE, QK_ROPE_DIM)),
        'cache_mem_k': nrm(ks[4], (DEPTH, DEC_BATCH, N_MEM, MEM_HEADS, MEM_HEAD_DIM)),
        'cache_mem_v': nrm(ks[5], (DEPTH, DEC_BATCH, N_MEM, MEM_HEADS, MEM_HEAD_DIM)),
        'state_pool': nrm(ks[6], (DEPTH, DEC_BATCH, POOL_CTX, POOL_WIDTH)),
        'page_table': page_table,
        'mem_prompt': nrm(ks[8], (BATCH, N_MEM, D_MODEL)),
        'w_in': nrm(ks[9], (DEPTH, D_MODEL, D_IN), D_MODEL ** -0.5),
        'g_q': 1.0 + nrm(ks[10], (DEPTH, Q_RANK), 0.05),
        'g_kv': 1.0 + nrm(ks[11], (DEPTH, KV_RANK), 0.05),
        'w_uq': nrm(ks[12], (DEPTH, Q_RANK, MLA_HEADS * (QK_NOPE_DIM + QK_ROPE_DIM)), Q_RANK ** -0.5),
        'w_uk': nrm(ks[13], (DEPTH, KV_RANK, MLA_HEADS, QK_NOPE_DIM), KV_RANK ** -0.5),
        'w_uv': nrm(ks[14], (DEPTH, KV_RANK, MLA_HEADS, V_HEAD_DIM), KV_RANK ** -0.5),
        'w_pool': nrm(ks[15], (DEPTH, POOL_GROUPS, POOL_GROUP_DIM, POOL_GROUP_DIM), POOL_GROUP_DIM ** -0.5),
        'pool_scale': 1.0 + nrm(ks[16], (DEPTH, POOL_WIDTH), 0.1),
        'w_o': nrm(ks[17], (DEPTH, D_MIX, D_MODEL), BETA * D_MIX ** -0.5),
        'ln1_g': 1.0 + nrm(ks[18], (DEPTH, D_MODEL), 0.05),
        'ln1_b': nrm(ks[19], (DEPTH, D_MODEL), 0.02),
        'w_mq': nrm(ks[20], (DEPTH, D_MODEL, D_MODEL), D_MODEL ** -0.5),
        'w_mk': nrm(ks[21], (DEPTH, D_MODEL, D_MODEL), D_MODEL ** -0.5),
        'w_mv': nrm(ks[22], (DEPTH, D_MODEL, D_MODEL), D_MODEL ** -0.5),
        'w_mo': nrm(ks[23], (DEPTH, D_MODEL, D_MODEL), BETA * D_MODEL ** -0.5),
        'ln2_g': 1.0 + nrm(ks[24], (DEPTH, D_MODEL), 0.05),
        'ln2_b': nrm(ks[25], (DEPTH, D_MODEL), 0.02),
        'w1': nrm(ks[26], (DEPTH, D_MODEL, D_FF), D_MODEL ** -0.5),
        'w2': nrm(ks[27], (DEPTH, D_FF, D_MODEL), BETA * D_FF ** -0.5),
        'ln3_g': 1.0 + nrm(ks[28], (DEPTH, D_MODEL), 0.05),
        'ln3_b': nrm(ks[29], (DEPTH, D_MODEL), 0.02),
    }


def reference(x_prompt, x_sample, cache_ckv, cache_krope, cache_mem_k, cache_mem_v, state_pool, page_table,
              mem_prompt, w_in, g_q, g_kv, w_uq, w_uk, w_uv, w_pool, pool_scale, w_o, ln1_g, ln1_b,
              w_mq, w_mk, w_mv, w_mo, ln2_g, ln2_b, w1, w2, ln3_g, ln3_b):
    B, S, _ = x_prompt.shape
    DB, T, _ = x_sample.shape
    n_pages = page_table.shape[1]
    past_len = n_pages * PAGE_SIZE
    pos_p = jnp.arange(S, dtype=jnp.int32)
    pos_s = past_len + jnp.arange(T, dtype=jnp.int32)
    xp, xs = x_prompt, x_sample
    ckv_p_l, kr_p_l, mk_p_l, mv_p_l, pst_p_l = [], [], [], [], []
    ckv_s_l, kr_s_l, pst_s_l = [], [], []
    for l in range(DEPTH):
        mix_w = (w_in[l], g_q[l], g_kv[l], w_uq[l], w_uk[l], w_uv[l], w_pool[l], pool_scale[l], w_o[l])
        zero_ctx = jnp.zeros((B, POOL_CTX, POOL_WIDTH), xp.dtype)
        y_p, pst_p, ckv_p, kr_p = token_mixer(xp, pos_p, zero_ctx, None, *mix_w)
        xp = layer_norm(ALPHA * xp + y_p, ln1_g[l], ln1_b[l])
        ckv_past = cache_ckv[l, page_table].reshape(DB, past_len, KV_RANK)
        kr_past = cache_krope[l, page_table].reshape(DB, past_len, QK_ROPE_DIM)
        y_s, pst_s, ckv_s, kr_s = token_mixer(xs, pos_s, state_pool[l], (ckv_past, kr_past), *mix_w)
        xs = layer_norm(ALPHA * xs + y_s, ln1_g[l], ln1_b[l])
        mk_p = (mem_prompt @ w_mk[l]).reshape(B, N_MEM, MEM_HEADS, MEM_HEAD_DIM)
        mv_p = (mem_prompt @ w_mv[l]).reshape(B, N_MEM, MEM_HEADS, MEM_HEAD_DIM)
        xp = layer_norm(ALPHA * xp + mem_attend(xp, mk_p, mv_p, w_mq[l], w_mo[l]), ln2_g[l], ln2_b[l])
        xs = layer_norm(ALPHA * xs + mem_attend(xs, cache_mem_k[l], cache_mem_v[l], w_mq[l], w_mo[l]), ln2_g[l], ln2_b[l])
        xp = layer_norm(ALPHA * xp + sq_relu_mlp(xp, w1[l], w2[l]), ln3_g[l], ln3_b[l])
        xs = layer_norm(ALPHA * xs + sq_relu_mlp(xs, w1[l], w2[l]), ln3_g[l], ln3_b[l])
        ckv_p_l.append(ckv_p)
        kr_p_l.append(kr_p)
        mk_p_l.append(mk_p)
        mv_p_l.append(mv_p)
        pst_p_l.append(pst_p)
        ckv_s_l.append(ckv_s)
        kr_s_l.append(kr_s)
        pst_s_l.append(pst_s)
    return (xp, xs,
            jnp.stack(ckv_p_l), jnp.stack(kr_p_l), jnp.stack(mk_p_l), jnp.stack(mv_p_l), jnp.stack(pst_p_l),
            jnp.stack(ckv_s_l), jnp.stack(kr_s_l), jnp.stack(pst_s_l))
```

```python
import functools

import jax
import jax.numpy as jnp
import numpy as np
from jax import lax
from jax.experimental import pallas as pl
from jax.experimental.pallas import tpu as pltpu

POOL_WINDOWS = (2, 4, 8, 16)
POOL_GROUPS = len(POOL_WINDOWS)
POOL_CTX = max(POOL_WINDOWS) - 1
MLA_HEADS = 8
QK_NOPE_DIM = 64
QK_ROPE_DIM = 32
V_HEAD_DIM = 64
ROPE_THETA = 10000.0
SM_SCALE = (QK_NOPE_DIM + QK_ROPE_DIM) ** -0.5
MEM_HEADS = 4
LN_EPS = 1e-5
RMS_EPS = 1e-6

LANES = 128
HALO = 16
VMEM_LIMIT = 56 * 1024 * 1024

ROPE_LANE0 = 64
ROPE_ROLL = LANES - QK_ROPE_DIM

BF16 = jnp.bfloat16
F32 = jnp.float32


def _dot(a, b):
    return jnp.dot(a, b, preferred_element_type=F32)


def _dot_nt(a, b):
    return lax.dot_general(a, b, (((1,), (1,)), ((), ())), preferred_element_type=F32)


def _layer_norm(z, g, b):
    mu = jnp.mean(z, axis=-1, keepdims=True)
    zc = z - mu
    var = jnp.mean(zc * zc, axis=-1, keepdims=True)
    return zc * lax.rsqrt(var + LN_EPS) * g + b


def _rms_norm(x, g):
    return x * lax.rsqrt(jnp.mean(x * x, axis=-1, keepdims=True) + RMS_EPS) * g


def _rope_tile(t, t1):
    a = t * t1
    return a + pltpu.roll(a, ROPE_ROLL, axis=1)


def _mla_project(h, t1, gq, gkv, w_uq_ref, w_uk_ref, q_ref, k_ref, ckv_ref, kr_ref, alpha):
    cqn = _rms_norm(h[:, 512:768], gq).astype(BF16)
    q = _dot(cqn, w_uq_ref[...])
    for hd in range(MLA_HEADS):
        qh = q[:, hd * LANES:(hd + 1) * LANES]
        q_lat = _dot(qh.astype(BF16), w_uk_ref[hd])
        q_ref[hd, :, 0:LANES] = (q_lat * alpha).astype(BF16)
        q_ref[hd, :, LANES:2 * LANES] = (_rope_tile(qh, t1) * alpha).astype(BF16)
    ckv = _rms_norm(h[:, 768:896], gkv)
    ckv_ref[...] = ckv
    ks = _rope_tile(h[:, 896:1024], t1)
    kr_ref[...] = ks
    lane = lax.broadcasted_iota(jnp.int32, ks.shape, 1)
    k2 = jnp.where((lane >= ROPE_LANE0) & (lane < ROPE_LANE0 + QK_ROPE_DIM), ks, 0.0)
    k_ref[:, 0:LANES] = ckv.astype(BF16)
    k_ref[:, LANES:2 * LANES] = k2.astype(BF16)


def _inproj_prompt_kernel(x_ref, xh_ref, t1_ref, w_in_ref, gq_ref, gkv_ref, w_uq_ref, w_uk_ref, w_pool_ref, ps_ref,
                          q_ref, k_ref, ckv_ref, kr_ref, pool_ref, tail_ref, *, tm):
    i = pl.program_id(1)
    xb = x_ref[...].astype(BF16)
    h = _dot(xb, w_in_ref[...])
    u = h[:, 0:512]
    uh = _dot(xh_ref[...].astype(BF16), w_in_ref[:, 0:512])
    uh = jnp.where(i == 0, 0.0, uh)
    pos = i * tm + lax.broadcasted_iota(jnp.int32, (tm, 1), 0)
    for g, w in enumerate(POOL_WINDOWS):
        c0, c1 = g * LANES, (g + 1) * LANES
        s = jnp.concatenate([uh[:, c0:c1], u[:, c0:c1]], axis=0)
        k = 1
        while k < w:
            s = s + pltpu.roll(s, k, axis=0)
            k *= 2
        cnt = jnp.minimum(w, pos + 1).astype(F32)
        pooled = (s[HALO:, :] / cnt - u[:, c0:c1]).astype(BF16)
        y = _dot(pooled, w_pool_ref[g]) * ps_ref[:, c0:c1]
        pool_ref[:, c0:c1] = y.astype(BF16)

    @pl.when(i == pl.num_programs(1) - 1)
    def _():
        tail_ref[...] = u[tm - HALO:, :]

    _mla_project(h, t1_ref[...], gq_ref[...], gkv_ref[...], w_uq_ref, w_uk_ref, q_ref, k_ref, ckv_ref, kr_ref, SM_SCALE)


def _inproj_prompt(x, t1, w_in, gq, gkv, w_uq, w_uk, w_pool, ps, *, tm):
    B, S, D = x.shape
    nt = S // tm
    r = tm // HALO
    full = lambda *shape: pl.BlockSpec(shape, lambda b, i: (0,) * len(shape))
    return pl.pallas_call(
        functools.partial(_inproj_prompt_kernel, tm=tm),
        grid=(B, nt),
        in_specs=[
            pl.BlockSpec((None, tm, D), lambda b, i: (b, i, 0)),
            pl.BlockSpec((None, HALO, D), lambda b, i: (b, jnp.maximum(i * r - 1, 0), 0)),
            pl.BlockSpec((tm, LANES), lambda b, i: (i, 0)),
            full(D, 1024), full(1, 256), full(1, 128), full(256, 1024), full(MLA_HEADS, LANES, LANES),
            full(POOL_GROUPS, LANES, LANES), full(1, 512),
        ],
        out_specs=[
            pl.BlockSpec((None, MLA_HEADS, tm, 2 * LANES), lambda b, i: (b, 0, i, 0)),
            pl.BlockSpec((None, tm, 2 * LANES), lambda b, i: (b, i, 0)),
            pl.BlockSpec((None, tm, LANES), lambda b, i: (b, i, 0)),
            pl.BlockSpec((None, tm, LANES), lambda b, i: (b, i, 0)),
            pl.BlockSpec((None, tm, 512), lambda b, i: (b, i, 0)),
            pl.BlockSpec((None, HALO, 512), lambda b, i: (b, 0, 0)),
        ],
        out_shape=[
            jax.ShapeDtypeStruct((B, MLA_HEADS, S, 2 * LANES), BF16),
            jax.ShapeDtypeStruct((B, S, 2 * LANES), BF16),
            jax.ShapeDtypeStruct((B, S, LANES), F32),
            jax.ShapeDtypeStruct((B, S, LANES), F32),
            jax.ShapeDtypeStruct((B, S, 512), BF16),
            jax.ShapeDtypeStruct((B, HALO, 512), F32),
        ],
        compiler_params=pltpu.CompilerParams(dimension_semantics=("parallel", "arbitrary"), vmem_limit_bytes=VMEM_LIMIT),
        name="inproj_prompt",
    )(x, x, t1, w_in, gq, gkv, w_uq, w_uk, w_pool, ps)


def _inproj_sample_kernel(x_ref, st_ref, t1_ref, w_in_ref, gq_ref, gkv_ref, w_uq_ref, w_uk_ref, w_pool_ref, ps_ref,
                          q_ref, k_ref, ckv_ref, kr_ref, pool_ref, u_ref, *, db, nt, past_len):
    xb = x_ref[...].astype(BF16)
    h = _dot(xb, w_in_ref[...])
    u = h[:, 0:512]
    u_ref[...] = u

    def ext(j, c0, c1):
        if j < POOL_CTX:
            return st_ref[j, :, c0:c1]
        return u[(j - POOL_CTX) * db:(j - POOL_CTX + 1) * db, c0:c1]

    for g, w in enumerate(POOL_WINDOWS):
        c0, c1 = g * LANES, (g + 1) * LANES
        for t in range(nt):
            s = ext(POOL_CTX + t, c0, c1)
            for k in range(1, w):
                s = s + ext(POOL_CTX + t - k, c0, c1)
            cnt = float(min(w, past_len + t + 1))
            pooled = (s / cnt - u[t * db:(t + 1) * db, c0:c1]).astype(BF16)
            y = _dot(pooled, w_pool_ref[g]) * ps_ref[:, c0:c1]
            pool_ref[t * db:(t + 1) * db, c0:c1] = y.astype(BF16)

    _mla_project(h, t1_ref[...], gq_ref[...], gkv_ref[...], w_uq_ref, w_uk_ref, q_ref, k_ref, ckv_ref, kr_ref, SM_SCALE)


def _inproj_sample(x_t, st_t, t1, w_in, gq, gkv, w_uq, w_uk, w_pool, ps, *, db, nt, past_len):
    n = x_t.shape[0]
    return pl.pallas_call(
        functools.partial(_inproj_sample_kernel, db=db, nt=nt, past_len=past_len),
        out_shape=[
            jax.ShapeDtypeStruct((MLA_HEADS, n, 2 * LANES), BF16),
            jax.ShapeDtypeStruct((n, 2 * LANES), BF16),
            jax.ShapeDtypeStruct((n, LANES), F32),
            jax.ShapeDtypeStruct((n, LANES), F32),
            jax.ShapeDtypeStruct((n, 512), BF16),
            jax.ShapeDtypeStruct((n, 512), F32),
        ],
        compiler_params=pltpu.CompilerParams(vmem_limit_bytes=VMEM_LIMIT),
        name="inproj_sample",
    )(x_t, st_t, t1, w_in, gq, gkv, w_uq, w_uk, w_pool, ps)


def _attn_prompt_kernel(q_ref, k_ref, o_ref, m_sc, l_sc, acc_sc, *, tq):
    qi = pl.program_id(1)
    rows = MLA_HEADS * tq
    q = q_ref[...].reshape(rows, 2 * LANES)
    m_sc[...] = jnp.full((rows, 1), -jnp.inf, F32)
    l_sc[...] = jnp.zeros((rows, 1), F32)
    acc_sc[...] = jnp.zeros((rows, LANES), F32)

    def step(j, masked):
        start = pl.multiple_of(j * tq, tq)
        kb = k_ref[pl.ds(start, tq), :]
        s = _dot_nt(q, kb)
        if masked:
            r = lax.broadcasted_iota(jnp.int32, (tq, tq), 0)
            c = lax.broadcasted_iota(jnp.int32, (tq, tq), 1)
            s = jnp.where((c <= r)[None], s.reshape(MLA_HEADS, tq, tq), -jnp.inf).reshape(rows, tq)
        m_old = m_sc[...]
        m_new = jnp.maximum(m_old, jnp.max(s, axis=-1, keepdims=True))
        a = jnp.exp(m_old - m_new)
        p = jnp.exp(s - m_new)
        l_sc[...] = a * l_sc[...] + jnp.sum(p, axis=-1, keepdims=True)
        acc_sc[...] = a * acc_sc[...] + _dot(p.astype(BF16), kb[:, 0:LANES])
        m_sc[...] = m_new

    def body(j, carry):
        step(j, False)
        return carry

    lax.fori_loop(0, qi, body, 0)
    step(qi, True)
    o = acc_sc[...] / l_sc[...]
    for hd in range(MLA_HEADS):
        o_ref[:, hd * LANES:(hd + 1) * LANES] = o[hd * tq:(hd + 1) * tq, :].astype(BF16)


def _attn_prompt(q, k, *, tq):
    B, H, S, W = q.shape
    rows = H * tq
    return pl.pallas_call(
        functools.partial(_attn_prompt_kernel, tq=tq),
        grid=(B, S // tq),
        in_specs=[
            pl.BlockSpec((None, H, tq, W), lambda b, i: (b, 0, i, 0)),
            pl.BlockSpec((None, S, W), lambda b, i: (b, 0, 0)),
        ],
        out_specs=pl.BlockSpec((None, tq, H * LANES), lambda b, i: (b, i, 0)),
        out_shape=jax.ShapeDtypeStruct((B, S, H * LANES), BF16),
        scratch_shapes=[pltpu.VMEM((rows, 1), F32), pltpu.VMEM((rows, 1), F32), pltpu.VMEM((rows, LANES), F32)],
        compiler_params=pltpu.CompilerParams(dimension_semantics=("parallel", "parallel"), vmem_limit_bytes=VMEM_LIMIT),
        name="attn_prompt",
    )(q, k)


def _attn_sample_kernel(pt_ref, q_ref, kn_ref, *refs, pages, nt):
    ckv_refs = refs[:pages]
    kr_refs = refs[pages:2 * pages]
    o_ref, m_sc, l_sc, acc_sc = refs[2 * pages:]
    j = pl.program_id(1)
    rows = MLA_HEADS * nt

    @pl.when(j == 0)
    def _():
        m_sc[...] = jnp.full((rows, 1), -jnp.inf, F32)
        l_sc[...] = jnp.zeros((rows, 1), F32)
        acc_sc[...] = jnp.zeros((rows, LANES), F32)

    q = q_ref[...]
    q_lat = q[:, 0:LANES]
    q_r = q[:, LANES + ROPE_LANE0:LANES + ROPE_LANE0 + QK_ROPE_DIM]

    def update(s, v_list):
        m_old = m_sc[...]
        m_new = jnp.maximum(m_old, jnp.max(s, axis=-1, keepdims=True))
        a = jnp.exp(m_old - m_new)
        p = jnp.exp(s - m_new)
        l_sc[...] = a * l_sc[...] + jnp.sum(p, axis=-1, keepdims=True)
        pv = None
        off = 0
        for v in v_list:
            n = v.shape[0]
            d = _dot(p[:, off:off + n].astype(BF16), v)
            pv = d if pv is None else pv + d
            off += n
        acc_sc[...] = a * acc_sc[...] + pv
        m_sc[...] = m_new

    cks, ss = [], []
    for p in range(pages):
        ck = ckv_refs[p][...].astype(BF16)
        kr = kr_refs[p][...].astype(BF16)
        ss.append(_dot_nt(q_lat, ck) + _dot_nt(q_r, kr))
        cks.append(ck)
    update(jnp.concatenate(ss, axis=1), cks)

    @pl.when(j == pl.num_programs(1) - 1)
    def _():
        kn = kn_ref[...]
        s = _dot_nt(q, kn)
        tq_ = lax.broadcasted_iota(jnp.int32, s.shape, 0) % nt
        tk_ = lax.broadcasted_iota(jnp.int32, s.shape, 1)
        s = jnp.where((tk_ <= tq_) & (tk_ < nt), s, -jnp.inf)
        update(s, [kn[:, 0:LANES]])
        o_ref[...] = (acc_sc[...] / l_sc[...]).astype(BF16)


def _attn_sample(pt_flat, q, kn, cache_ckv, cache_krope, layer, *, n_pages, pages, nt):
    DB, rows, W = q.shape
    page = cache_ckv.shape[2]
    nj = n_pages // pages

    def ck_spec(p):
        return pl.BlockSpec((None, None, page, LANES),
                            lambda b, j, pt: (layer, pt[b * n_pages + j * pages + p], 0, 0))

    def kr_spec(p):
        return pl.BlockSpec((None, None, page, QK_ROPE_DIM),
                            lambda b, j, pt: (layer, pt[b * n_pages + j * pages + p], 0, 0))

    grid_spec = pltpu.PrefetchScalarGridSpec(
        num_scalar_prefetch=1,
        grid=(DB, nj),
        in_specs=[pl.BlockSpec((None, rows, W), lambda b, j, pt: (b, 0, 0)),
                  pl.BlockSpec((None, 8, W), lambda b, j, pt: (b, 0, 0))]
                 + [ck_spec(p) for p in range(pages)] + [kr_spec(p) for p in range(pages)],
        out_specs=pl.BlockSpec((None, rows, LANES), lambda b, j, pt: (b, 0, 0)),
        scratch_shapes=[pltpu.VMEM((rows, 1), F32), pltpu.VMEM((rows, 1), F32), pltpu.VMEM((rows, LANES), F32)],
    )
    return pl.pallas_call(
        functools.partial(_attn_sample_kernel, pages=pages, nt=nt),
        grid_spec=grid_spec,
        out_shape=jax.ShapeDtypeStruct((DB, rows, LANES), BF16),
        compiler_params=pltpu.CompilerParams(dimension_semantics=("parallel", "arbitrary"), vmem_limit_bytes=VMEM_LIMIT),
        name="attn_sample",
    )(pt_flat, q, kn, *([cache_ckv] * pages), *([cache_krope] * pages))


def _outproj_kernel(ol_ref, pool_ref, x_ref, w_uv_ref, w_o_ref, g_ref, b_ref, y_ref, *, alpha):
    y = _dot(pool_ref[...], w_o_ref[0:512, :])
    for pr in range(MLA_HEADS // 2):
        o2 = _dot(ol_ref[:, pr * 2 * LANES:(pr + 1) * 2 * LANES], w_uv_ref[pr]).astype(BF16)
        y = y + _dot(o2, w_o_ref[512 + pr * LANES:512 + (pr + 1) * LANES, :])
    y_ref[...] = _layer_norm(alpha * x_ref[...] + y, g_ref[...], b_ref[...])


def _outproj(o_lat, pool_y, x, w_uv_bd, w_o, g, b, *, tm, alpha):
    N, D = x.shape
    full = lambda *shape: pl.BlockSpec(shape, lambda i: (0,) * len(shape))
    return pl.pallas_call(
        functools.partial(_outproj_kernel, alpha=alpha),
        grid=(N // tm,),
        in_specs=[
            pl.BlockSpec((tm, MLA_HEADS * LANES), lambda i: (i, 0)),
            pl.BlockSpec((tm, 512), lambda i: (i, 0)),
            pl.BlockSpec((tm, D), lambda i: (i, 0)),
            full(MLA_HEADS // 2, 2 * LANES, LANES), full(1024, D), full(1, D), full(1, D),
        ],
        out_specs=pl.BlockSpec((tm, D), lambda i: (i, 0)),
        out_shape=jax.ShapeDtypeStruct((N, D), F32),
        compiler_params=pltpu.CompilerParams(dimension_semantics=("parallel",), vmem_limit_bytes=VMEM_LIMIT),
        name="outproj_ln",
    )(o_lat, pool_y, x, w_uv_bd, w_o, g, b)


def _mem_heads(q, mk_of, mv_of, hd_dim):
    outs = []
    for hd in range(MEM_HEADS):
        qh = q[:, hd * hd_dim:(hd + 1) * hd_dim].astype(BF16)
        s = _dot_nt(qh, mk_of(hd))
        e = jnp.exp(s - jnp.max(s, axis=-1, keepdims=True))
        p = e / jnp.sum(e, axis=-1, keepdims=True)
        outs.append(_dot(p.astype(BF16), mv_of(hd)).astype(BF16))
    return jnp.concatenate(outs, axis=-1)


def _memattn_prompt_kernel(x_ref, mk_ref, mv_ref, w_mq_ref, w_mo_ref, g_ref, b_ref, y_ref, *, alpha, hd_dim):
    x = x_ref[...]
    q = _dot(x.astype(BF16), w_mq_ref[...]) * (hd_dim ** -0.5)
    o = _mem_heads(q, lambda h: mk_ref[:, h * hd_dim:(h + 1) * hd_dim], lambda h: mv_ref[:, h * hd_dim:(h + 1) * hd_dim], hd_dim)
    y = _dot(o, w_mo_ref[...])
    y_ref[...] = _layer_norm(alpha * x + y, g_ref[...], b_ref[...])


def _memattn_prompt(x, mk, mv, w_mq, w_mo, g, b, *, tm, alpha):
    B, S, D = x.shape
    n_mem = mk.shape[1]
    full = lambda *shape: pl.BlockSpec(shape, lambda bb, i: (0,) * len(shape))
    return pl.pallas_call(
        functools.partial(_memattn_prompt_kernel, alpha=alpha, hd_dim=D // MEM_HEADS),
        grid=(B, S // tm),
        in_specs=[
            pl.BlockSpec((None, tm, D), lambda bb, i: (bb, i, 0)),
            pl.BlockSpec((None, n_mem, D), lambda bb, i: (bb, 0, 0)),
            pl.BlockSpec((None, n_mem, D), lambda bb, i: (bb, 0, 0)),
            full(D, D), full(D, D), full(1, D), full(1, D),
        ],
        out_specs=pl.BlockSpec((None, tm, D), lambda bb, i: (bb, i, 0)),
        out_shape=jax.ShapeDtypeStruct((B, S, D), F32),
        compiler_params=pltpu.CompilerParams(dimension_semantics=("parallel", "parallel"), vmem_limit_bytes=VMEM_LIMIT),
        name="memattn_prompt",
    )(x, mk, mv, w_mq, w_mo, g, b)


def _memattn_sample_kernel(x_ref, mk_ref, mv_ref, w_mq_ref, w_mo_ref, g_ref, b_ref, y_ref, o_sc, *, alpha, hd_dim, bb, nt):
    x = x_ref[...]
    q = _dot(x.astype(BF16), w_mq_ref[...]) * (hd_dim ** -0.5)
    for e in range(bb):
        qe = q[e * nt:(e + 1) * nt, :]
        o_sc[e * nt:(e + 1) * nt, :] = _mem_heads(
            qe,
            lambda h: mk_ref[e, :, h * hd_dim:(h + 1) * hd_dim].astype(BF16),
            lambda h: mv_ref[e, :, h * hd_dim:(h + 1) * hd_dim].astype(BF16), hd_dim).astype(F32)
    y = _dot(o_sc[...].astype(BF16), w_mo_ref[...])
    y_ref[...] = _layer_norm(alpha * x + y, g_ref[...], b_ref[...])


def _memattn_sample(x, mem_k, mem_v, layer, w_mq, w_mo, g, b, *, bb, nt, alpha):
    N, D = x.shape
    n_mem = mem_k.shape[2]
    full = lambda *shape: pl.BlockSpec(shape, lambda i: (0,) * len(shape))
    return pl.pallas_call(
        functools.partial(_memattn_sample_kernel, alpha=alpha, hd_dim=D // MEM_HEADS, bb=bb, nt=nt),
        grid=(N // (bb * nt),),
        in_specs=[
            pl.BlockSpec((bb * nt, D), lambda i: (i, 0)),
            pl.BlockSpec((None, bb, n_mem, D), lambda i: (layer, i, 0, 0)),
            pl.BlockSpec((None, bb, n_mem, D), lambda i: (layer, i, 0, 0)),
            full(D, D), full(D, D), full(1, D), full(1, D),
        ],
        out_specs=pl.BlockSpec((bb * nt, D), lambda i: (i, 0)),
        out_shape=jax.ShapeDtypeStruct((N, D), F32),
        scratch_shapes=[pltpu.VMEM((bb * nt, D), F32)],
        compiler_params=pltpu.CompilerParams(dimension_semantics=("parallel",), vmem_limit_bytes=VMEM_LIMIT),
        name="memattn_sample",
    )(x, mem_k, mem_v, w_mq, w_mo, g, b)


def _mlp_kernel(x_ref, w1_ref, w2_ref, g_ref, b_ref, y_ref, xb_sc, acc_sc, *, alpha):
    j = pl.program_id(1)

    @pl.when(j == 0)
    def _():
        xb_sc[...] = x_ref[...].astype(BF16)
        acc_sc[...] = jnp.zeros_like(acc_sc)

    h = jnp.maximum(_dot(xb_sc[...], w1_ref[...]), 0.0)
    acc_sc[...] += _dot((h * h).astype(BF16), w2_ref[...])

    @pl.when(j == pl.num_programs(1) - 1)
    def _():
        y_ref[...] = _layer_norm(alpha * x_ref[...] + acc_sc[...], g_ref[...], b_ref[...])


def _mlp(x, w1, w2, g, b, *, tm, tf, alpha):
    N, D = x.shape
    F = w1.shape[1]
    return pl.pallas_call(
        functools.partial(_mlp_kernel, alpha=alpha),
        grid=(N // tm, F // tf),
        in_specs=[
            pl.BlockSpec((tm, D), lambda i, j: (i, 0)),
            pl.BlockSpec((D, tf), lambda i, j: (0, j)),
            pl.BlockSpec((tf, D), lambda i, j: (j, 0)),
            pl.BlockSpec((1, D), lambda i, j: (0, 0)),
            pl.BlockSpec((1, D), lambda i, j: (0, 0)),
        ],
        out_specs=pl.BlockSpec((tm, D), lambda i, j: (i, 0)),
        out_shape=jax.ShapeDtypeStruct((N, D), F32),
        scratch_shapes=[pltpu.VMEM((tm, D), BF16), pltpu.VMEM((tm, D), F32)],
        compiler_params=pltpu.CompilerParams(dimension_semantics=("parallel", "arbitrary"), vmem_limit_bytes=VMEM_LIMIT),
        name="mlp_ln",
    )(x, w1, w2, g, b)


def _memkv_kernel(m_ref, wk_ref, wv_ref, k_ref, v_ref, kb_ref, vb_ref):
    mb = m_ref[...].astype(BF16)
    k = _dot(mb, wk_ref[...])
    v = _dot(mb, wv_ref[...])
    k_ref[...] = k
    v_ref[...] = v
    kb_ref[...] = k.astype(BF16)
    vb_ref[...] = v.astype(BF16)


def _memkv(m, wk, wv, *, tm):
    N, D = m.shape
    row = pl.BlockSpec((tm, D), lambda i: (i, 0))
    wsp = pl.BlockSpec((D, D), lambda i: (0, 0))
    return pl.pallas_call(
        _memkv_kernel,
        grid=(N // tm,),
        in_specs=[row, wsp, wsp],
        out_specs=[row, row, row, row],
        out_shape=[jax.ShapeDtypeStruct((N, D), F32), jax.ShapeDtypeStruct((N, D), F32),
                   jax.ShapeDtypeStruct((N, D), BF16), jax.ShapeDtypeStruct((N, D), BF16)],
        compiler_params=pltpu.CompilerParams(dimension_semantics=("parallel",), vmem_limit_bytes=VMEM_LIMIT),
        name="mem_kv_proj",
    )(m, wk, wv)


def _rope_partner(w):
    half = QK_ROPE_DIM // 2
    return jnp.concatenate([-w[..., half:], w[..., :half]], axis=-1)


def _prep_w_in(w_in):
    d = w_in.shape[0]
    kr = w_in[:, 896:928]
    return jnp.concatenate([w_in[:, :896], jnp.zeros((d, ROPE_LANE0), w_in.dtype), kr, _rope_partner(kr)], axis=1).astype(BF16)


def _prep_w_uq(w_uq):
    r = w_uq.shape[0]
    w = w_uq.reshape(r, MLA_HEADS, QK_NOPE_DIM + QK_ROPE_DIM)
    rope = w[:, :, QK_NOPE_DIM:]
    return jnp.concatenate([w[:, :, :QK_NOPE_DIM], rope, _rope_partner(rope)], axis=-1).reshape(r, MLA_HEADS * LANES).astype(BF16)


def _prep_w_uk(w_uk):
    wt = jnp.transpose(w_uk, (1, 2, 0))
    return jnp.concatenate([wt, jnp.zeros_like(wt)], axis=1).astype(BF16)


def _prep_w_uv(w_uv):
    c = w_uv.shape[0]
    z = jnp.zeros((c, V_HEAD_DIM), w_uv.dtype)
    blocks = []
    for pr in range(MLA_HEADS // 2):
        top = jnp.concatenate([w_uv[:, 2 * pr, :], z], axis=1)
        bot = jnp.concatenate([z, w_uv[:, 2 * pr + 1, :]], axis=1)
        blocks.append(jnp.concatenate([top, bot], axis=0))
    return jnp.stack(blocks).astype(BF16)


def _rope_table(pos):
    half = QK_ROPE_DIM // 2
    inv_freq = ROPE_THETA ** (-jnp.arange(half, dtype=F32) / half)
    ang = pos.astype(F32)[:, None] * inv_freq[None, :]
    cos, sin = jnp.cos(ang), jnp.sin(ang)
    return jnp.concatenate([jnp.zeros((pos.shape[0], ROPE_LANE0), F32), cos, cos, sin, sin], axis=1)


def kernel(x_prompt, x_sample, cache_ckv, cache_krope, cache_mem_k, cache_mem_v, state_pool, page_table, mem_prompt, w_in, g_q, g_kv, w_uq, w_uk, w_uv, w_pool, pool_scale, w_o, ln1_g, ln1_b, w_mq, w_mk, w_mv, w_mo, ln2_g, ln2_b, w1, w2, ln3_g, ln3_b):
    B, S, D = x_prompt.shape
    DB, T, _ = x_sample.shape
    depth = w_in.shape[0]
    n_pages = page_table.shape[1]
    page = cache_ckv.shape[2]
    past_len = n_pages * page
    n_mem = mem_prompt.shape[1]
    alpha = (2.0 * depth) ** 0.25
    Np, Ns = B * S, DB * T

    tm_p = min(512, S)
    tq = min(512, S)
    tm_s = Ns
    pages = min(16, n_pages)
    bb = min(8, DB)

    t1_p = _rope_table(jnp.arange(S, dtype=jnp.int32))
    t1_s = jnp.repeat(_rope_table(past_len + jnp.arange(T, dtype=jnp.int32)), DB, axis=0)
    pt_flat = page_table.reshape(-1).astype(jnp.int32)
    mem_k_s = cache_mem_k.reshape(depth, DB, n_mem, D)
    mem_v_s = cache_mem_v.reshape(depth, DB, n_mem, D)
    row = lambda v: v.reshape(1, -1)

    xp = x_prompt
    xs = x_sample.reshape(Ns, D)
    outs = {k: [] for k in ("ckv_p", "kr_p", "mk_p", "mv_p", "pst_p", "ckv_s", "kr_s", "pst_s")}
    for l in range(depth):
        w_in_l = _prep_w_in(w_in[l])
        w_uq_l = _prep_w_uq(w_uq[l])
        w_uk_l = _prep_w_uk(w_uk[l])
        w_uv_l = _prep_w_uv(w_uv[l])
        w_pool_l = w_pool[l].astype(BF16)
        w_o_l = w_o[l].astype(BF16)
        mix = (w_in_l, row(g_q[l]), row(g_kv[l]), w_uq_l, w_uk_l, w_pool_l, row(pool_scale[l]))

        q, k, ckv, krt, pool_y, tail = _inproj_prompt(xp, t1_p, *mix, tm=tm_p)
        o_lat = _attn_prompt(q, k, tq=tq)
        xp2 = _outproj(o_lat.reshape(Np, -1), pool_y.reshape(Np, -1), xp.reshape(Np, D), w_uv_l, w_o_l,
                       row(ln1_g[l]), row(ln1_b[l]), tm=tm_p, alpha=alpha)
        outs["ckv_p"].append(ckv)
        outs["kr_p"].append(krt[:, :, ROPE_LANE0:ROPE_LANE0 + QK_ROPE_DIM])
        outs["pst_p"].append(tail[:, HALO - POOL_CTX:, :])

        xs_t = xs.reshape(DB, T, D).transpose(1, 0, 2).reshape(Ns, D)
        st_t = state_pool[l].transpose(1, 0, 2)
        q_s, k_s, ckv_s, krt_s, pool_s, u_s = _inproj_sample(xs_t, st_t, t1_s, *mix, db=DB, nt=T, past_len=past_len)
        bm = lambda a: a.reshape(T, DB, -1).transpose(1, 0, 2)
        q_b = q_s.reshape(MLA_HEADS, T, DB, 2 * LANES).transpose(2, 0, 1, 3).reshape(DB, MLA_HEADS * T, 2 * LANES)
        kn = jnp.pad(bm(k_s), ((0, 0), (0, 8 - T), (0, 0)))
        o_s = _attn_sample(pt_flat, q_b, kn, cache_ckv, cache_krope, l, n_pages=n_pages, pages=pages, nt=T)
        o_s = o_s.reshape(DB, MLA_HEADS, T, LANES).transpose(0, 2, 1, 3).reshape(Ns, MLA_HEADS * LANES)
        xs2 = _outproj(o_s, bm(pool_s).reshape(Ns, -1), xs, w_uv_l, w_o_l, row(ln1_g[l]), row(ln1_b[l]), tm=tm_s, alpha=alpha)
        outs["ckv_s"].append(bm(ckv_s))
        outs["kr_s"].append(bm(krt_s)[:, :, ROPE_LANE0:ROPE_LANE0 + QK_ROPE_DIM])
        outs["pst_s"].append(jnp.concatenate([state_pool[l], bm(u_s)], axis=1)[:, -POOL_CTX:, :])

        w_mq_l, w_mo_l = w_mq[l].astype(BF16), w_mo[l].astype(BF16)
        mk, mv, mk_b, mv_b = _memkv(mem_prompt.reshape(B * n_mem, D), w_mk[l].astype(BF16), w_mv[l].astype(BF16),
                                    tm=min(512, B * n_mem))
        outs["mk_p"].append(mk.reshape(B, n_mem, MEM_HEADS, D // MEM_HEADS))
        outs["mv_p"].append(mv.reshape(B, n_mem, MEM_HEADS, D // MEM_HEADS))
        xp3 = _memattn_prompt(xp2.reshape(B, S, D), mk_b.reshape(B, n_mem, D), mv_b.reshape(B, n_mem, D), w_mq_l, w_mo_l,
                              row(ln2_g[l]), row(ln2_b[l]), tm=tm_p, alpha=alpha)
        xs3 = _memattn_sample(xs2, mem_k_s, mem_v_s, l, w_mq_l, w_mo_l, row(ln2_g[l]), row(ln2_b[l]), bb=bb, nt=T, alpha=alpha)

        w1_l, w2_l = w1[l].astype(BF16), w2[l].astype(BF16)
        tf = min(512, w1_l.shape[1])
        xp = _mlp(xp3.reshape(Np, D), w1_l, w2_l, row(ln3_g[l]), row(ln3_b[l]), tm=min(1024, Np), tf=tf, alpha=alpha).reshape(B, S, D)
        xs = _mlp(xs3, w1_l, w2_l, row(ln3_g[l]), row(ln3_b[l]), tm=tm_s, tf=tf, alpha=alpha)

    st = lambda name: jnp.stack(outs[name])
    return (xp, xs.reshape(DB, T, D), st("ckv_p"), st("kr_p"), st("mk_p"), st("mv_p"), st("pst_p"),
            st("ckv_s"), st("kr_s"), st("pst_s"))
```
